```python
import math
import jax
import jax.numpy as jnp
from jax import lax
import numpy as np

D_MODEL = 4096
BATCH = 1
SEQ = 8192
DEPTH = 2
DEC_BATCH = 32
DEC_SEQ = 32
PAST_LEN = 2048

CHUNK = 64
RET_HEADS = 8
RET_DK = 128
RET_DV = 256
RET_THETA = 10000.0
ATT_HEADS = 32
ATT_KV_HEADS = 4
ATT_HEAD_DIM = 64
WINDOW = 128
ROPE_THETA = 500000.0
ROPE_DIM = ATT_HEAD_DIM // 4
D_FF = 14336
N_EXPERTS = 8
TOP_K = 2
EPS = 1e-5
NEG_INF = -1e30

RET_QK_W = RET_HEADS * RET_DK
RET_V_W = RET_HEADS * RET_DV
ATT_Q_W = ATT_HEADS * ATT_HEAD_DIM
ATT_KV_W = ATT_KV_HEADS * ATT_HEAD_DIM
MIX_W = RET_V_W + ATT_Q_W
IN_SPLITS = (RET_QK_W, 2 * RET_QK_W, 2 * RET_QK_W + RET_V_W, 2 * RET_QK_W + 2 * RET_V_W,
             2 * RET_QK_W + 2 * RET_V_W + ATT_Q_W, 2 * RET_QK_W + 2 * RET_V_W + ATT_Q_W + ATT_KV_W)
IN_W = 2 * RET_QK_W + 2 * RET_V_W + ATT_Q_W + 2 * ATT_KV_W
N_DENSE = (DEPTH + 1) // 2
N_MOE = DEPTH // 2

kernel_name = 'hymba_retention_swa_streaming_step'


def rms_norm(x, g):
    xf = x.astype(jnp.float32)
    y = xf * lax.rsqrt(jnp.mean(xf * xf, axis=-1, keepdims=True) + EPS)
    return (y * g.astype(jnp.float32)).astype(x.dtype)


def rotary(x, pos, rot_dim, theta):
    half = rot_dim // 2
    inv_freq = jnp.exp(-math.log(theta) * jnp.arange(half, dtype=jnp.float32) / half)
    ang = pos.astype(jnp.float32)[:, None] * inv_freq[None, :]
    cos = jnp.cos(ang)[:, None, :]
    sin = jnp.sin(ang)[:, None, :]
    xf = x.astype(jnp.float32)
    x1 = xf[..., :half]
    x2 = xf[..., half:rot_dim]
    parts = [x1 * cos - x2 * sin, x2 * cos + x1 * sin]
    if rot_dim < x.shape[-1]:
        parts.append(xf[..., rot_dim:])
    return jnp.concatenate(parts, axis=-1).astype(x.dtype)


def project(h, w_in, pos):
    B, S, _ = h.shape
    z = jnp.einsum('bsd,de->bse', h, w_in)
    rq, rk, rv, rg, aq, ak, av = jnp.split(z, IN_SPLITS, axis=-1)
    rq = rotary(rq.reshape(B, S, RET_HEADS, RET_DK), pos, RET_DK, RET_THETA).astype(jnp.float32)
    rk = rotary(rk.reshape(B, S, RET_HEADS, RET_DK), pos, RET_DK, RET_THETA).astype(jnp.float32) * (RET_DK ** -0.5)
    rv = rv.reshape(B, S, RET_HEADS, RET_DV).astype(jnp.float32)
    aq = rotary(aq.reshape(B, S, ATT_HEADS, ATT_HEAD_DIM), pos, ROPE_DIM, ROPE_THETA)
    ak = rotary(ak.reshape(B, S, ATT_KV_HEADS, ATT_HEAD_DIM), pos, ROPE_DIM, ROPE_THETA)
    av = av.reshape(B, S, ATT_KV_HEADS, ATT_HEAD_DIM)
    return rq, rk, rv, rg, aq, ak, av


def retention_prompt(q, k, v, log_g):
    B, S, H, dk = q.shape
    n = S // CHUNK
    qc = q.reshape(B, n, CHUNK, H, dk)
    kc = k.reshape(B, n, CHUNK, H, dk)
    vc = v.reshape(B, n, CHUNK, H, RET_DV)
    idx = jnp.arange(CHUNK, dtype=jnp.float32)
    intra = jnp.exp(log_g[:, None, None] * jnp.abs(idx[:, None] - idx[None, :]))
    scores = jnp.einsum('bnihd,bnjhd->bnhij', qc, kc) * intra
    o_intra = jnp.einsum('bnhij,bnjhe->bnihe', scores, vc)
    w_end = jnp.exp(log_g[None, :] * (CHUNK - 1 - idx)[:, None])
    kv = jnp.einsum('bnjhd,jh,bnjhe->bnhde', kc, w_end, vc)
    g_chunk = jnp.exp(log_g * CHUNK)[None, :, None, None]

    def step(s, kv_c):
        return g_chunk * s + kv_c, s

    s0 = jnp.zeros((B, H, dk, RET_DV), jnp.float32)
    s_final, s_prev = lax.scan(step, s0, jnp.moveaxis(kv, 1, 0))
    s_prev = jnp.moveaxis(s_prev, 0, 1)
    w_in = jnp.exp(log_g[None, :] * (idx + 1.0)[:, None])
    o_cross = jnp.einsum('bnihd,bnhde->bnihe', qc, s_prev) * w_in[None, None, :, :, None]
    return (o_intra + o_cross).reshape(B, S, H, RET_DV), s_final


def retention_sample(q, k, v, state, log_g):
    L = q.shape[1]
    idx = jnp.arange(L, dtype=jnp.float32)
    intra = jnp.exp(log_g[:, None, None] * jnp.abs(idx[:, None] - idx[None, :]))
    scores = jnp.einsum('bihd,bjhd->bhij', q, k) * intra
    w_in = jnp.exp(log_g[None, :] * (idx + 1.0)[:, None])
    o = jnp.einsum('bhij,bjhe->bihe', scores, v) + jnp.einsum('bihd,bhde->bihe', q, state) * w_in[None, :, :, None]
    w_end = jnp.exp(log_g[None, :] * (L - 1.0 - idx)[:, None])
    s_new = jnp.exp(log_g * L)[None, :, None, None] * state + jnp.einsum('bjhd,jh,bjhe->bhde', k, w_end, v)
    return o, s_new


def sink_softmax(s, sinks):
    sk = jnp.broadcast_to(sinks.reshape(ATT_KV_HEADS, -1).astype(jnp.float32)[:, :, None, None], s.shape[:-1] + (1,))
    p = jax.nn.softmax(jnp.concatenate([s, sk], axis=-1), axis=-1)
    return p[..., :-1]


def swa_prompt(q, k, v, sinks):
    B, S, Hq, hd = q.shape
    n = S // CHUNK
    G = Hq // ATT_KV_HEADS
    nb = WINDOW // CHUNK
    qc = q.reshape(B, n, CHUNK, ATT_KV_HEADS, G, hd)
    pad = jnp.zeros((B, nb, CHUNK, ATT_KV_HEADS, hd), k.dtype)
    kp = jnp.concatenate([pad, k.reshape(B, n, CHUNK, ATT_KV_HEADS, hd)], axis=1)
    vp = jnp.concatenate([pad.astype(v.dtype), v.reshape(B, n, CHUNK, ATT_KV_HEADS, hd)], axis=1)
    kb = jnp.concatenate([kp[:, j:j + n] for j in range(nb + 1)], axis=2)
    vb = jnp.concatenate([vp[:, j:j + n] for j in range(nb + 1)], axis=2)
    s = jnp.einsum('bnqkgd,bnjkd->bnkgqj', qc, kb).astype(jnp.float32) * (hd ** -0.5)
    chunk_id = jnp.arange(n)[:, None] - nb + jnp.arange(nb + 1)[None, :]
    valid = jnp.repeat(chunk_id >= 0, CHUNK, axis=1)
    s = jnp.where(valid[None, :, None, None, None, :], s, NEG_INF)
    p = sink_softmax(s, sinks)
    o = jnp.einsum('bnkgqj,bnjkd->bnqkgd', p.astype(v.dtype), vb)
    return o.reshape(B, S, Hq * hd)


def swa_sample(q, k_new, v_new, cache_k, cache_v, sinks):
    Bd, L, Hq, hd = q.shape
    G = Hq // ATT_KV_HEADS
    kk = jnp.concatenate([cache_k.astype(k_new.dtype), k_new], axis=1)
    vv = jnp.concatenate([cache_v.astype(v_new.dtype), v_new], axis=1)
    qg = q.reshape(Bd, L, ATT_KV_HEADS, G, hd)
    s = jnp.einsum('bqkgd,bjkd->bkgqj', qg, kk).astype(jnp.float32) * (hd ** -0.5)
    p = sink_softmax(s, sinks)
    o = jnp.einsum('bkgqj,bjkd->bqkgd', p.astype(vv.dtype), vv).reshape(Bd, L, Hq * hd)
    win = cache_k.shape[1]
    return o, kk[:, -win:], vv[:, -win:]


def merge(ret_o, rg, gn_gain, att_o, w_out):
    B, S, H, dv = ret_o.shape
    mu = jnp.mean(ret_o, axis=-1, keepdims=True)
    var = jnp.mean(jnp.square(ret_o - mu), axis=-1, keepdims=True)
    y_ret = ((ret_o - mu) * lax.rsqrt(var + EPS)).reshape(B, S, H * dv) * gn_gain.astype(jnp.float32)
    y_ret = jax.nn.silu(rg.astype(jnp.float32)) * y_ret
    y = jnp.concatenate([y_ret.astype(w_out.dtype), att_o.astype(w_out.dtype)], axis=-1)
    return jnp.einsum('bse,ed->bsd', y, w_out)


def swiglu(h, wg, wu, wd):
    a = jnp.einsum('bsd,df->bsf', h, wg)
    b = jnp.einsum('bsd,df->bsf', h, wu)
    return jnp.einsum('bsf,fd->bsd', jax.nn.silu(a) * b, wd)


def moe_ffn(h, w_router, wg, wu, wd):
    logits = jnp.einsum('bsd,de->bse', h, w_router).astype(jnp.float32)
    top_v, top_i = lax.top_k(logits, TOP_K)
    gates = jax.nn.softmax(top_v, axis=-1)
    combine = jnp.sum(jax.nn.one_hot(top_i, N_EXPERTS, dtype=jnp.float32) * gates[..., None], axis=-2)
    out = jnp.zeros(h.shape, jnp.float32)
    for e in range(N_EXPERTS):
        out = out + combine[..., e:e + 1] * swiglu(h, wg[e], wu[e], wd[e]).astype(jnp.float32)
    return out.astype(h.dtype)


def setup_inputs(seed: int = 0) -> dict:
    key = jax.random.key(seed)
    ks = jax.random.split(key, 20)
    f32 = jnp.float32
    win = min(WINDOW, PAST_LEN)

    def nrm(k, shape, scale):
        return jax.random.normal(k, shape, f32) * scale

    return {
        'x_prompt': nrm(ks[0], (BATCH, SEQ, D_MODEL), 1.0),
        'x_sample': nrm(ks[1], (DEC_BATCH, DEC_SEQ, D_MODEL), 1.0),
        'cache_k': nrm(ks[2], (DEPTH, DEC_BATCH, win, ATT_KV_HEADS, ATT_HEAD_DIM), 1.0),
        'cache_v': nrm(ks[3], (DEPTH, DEC_BATCH, win, ATT_KV_HEADS, ATT_HEAD_DIM), 1.0),
        'state_ret': nrm(ks[4], (DEPTH, DEC_BATCH, RET_HEADS, RET_DK, RET_DV), 1.0),
        'ln1': 1.0 + nrm(ks[5], (DEPTH, D_MODEL), 0.02),
        'w_in': nrm(ks[6], (DEPTH, D_MODEL, IN_W), D_MODEL ** -0.5),
        'ret_gn': 1.0 + nrm(ks[7], (DEPTH, RET_V_W), 0.02),
        'att_sinks': nrm(ks[8], (DEPTH, ATT_HEADS), 0.5),
        'w_out': nrm(ks[9], (DEPTH, MIX_W, D_MODEL), MIX_W ** -0.5),
        'ln2': 1.0 + nrm(ks[10], (DEPTH, D_MODEL), 0.02),
        'ffn_w_gate': nrm(ks[11], (N_DENSE, D_MODEL, D_FF), D_MODEL ** -0.5),
        'ffn_w_up': nrm(ks[12], (N_DENSE, D_MODEL, D_FF), D_MODEL ** -0.5),
        'ffn_w_down': nrm(ks[13], (N_DENSE, D_FF, D_MODEL), D_FF ** -0.5),
        'router': nrm(ks[14], (N_MOE, D_MODEL, N_EXPERTS), D_MODEL ** -0.5),
        'moe_w_gate': nrm(ks[15], (N_MOE, N_EXPERTS, D_MODEL, D_FF), D_MODEL ** -0.5),
        'moe_w_up': nrm(ks[16], (N_MOE, N_EXPERTS, D_MODEL, D_FF), D_MODEL ** -0.5),
        'moe_w_down': nrm(ks[17], (N_MOE, N_EXPERTS, D_FF, D_MODEL), D_FF ** -0.5),
        'ln_f': 1.0 + nrm(ks[18], (D_MODEL,), 0.02),
    }


def reference(x_prompt, x_sample, cache_k, cache_v, state_ret, ln1, w_in, ret_gn, att_sinks, w_out,
              ln2, ffn_w_gate, ffn_w_up, ffn_w_down, router, moe_w_gate, moe_w_up, moe_w_down, ln_f):
    log_g = jnp.log(1.0 - jnp.exp2(-5.0 - jnp.arange(RET_HEADS, dtype=jnp.float32)))
    pos_p = jnp.arange(x_prompt.shape[1], dtype=jnp.int32)
    pos_s = PAST_LEN + jnp.arange(x_sample.shape[1], dtype=jnp.int32)
    xp, xs = x_prompt, x_sample
    pk, pv, ps, sk, sv, ss = [], [], [], [], [], []
    for l in range(DEPTH):
        hp = rms_norm(xp, ln1[l])
        rq, rk, rv, rg, aq, ak, av = project(hp, w_in[l], pos_p)
        ret_o, s_fin = retention_prompt(rq, rk, rv, log_g)
        att_o = swa_prompt(aq, ak, av, att_sinks[l])
        xp = xp + merge(ret_o, rg, ret_gn[l], att_o, w_out[l]).astype(xp.dtype)
        win_p = min(WINDOW, x_prompt.shape[1])
        pk.append(ak[:, -win_p:].astype(cache_k.dtype))
        pv.append(av[:, -win_p:].astype(cache_v.dtype))
        ps.append(s_fin.astype(state_ret.dtype))
        hs = rms_norm(xs, ln1[l])
        rq, rk, rv, rg, aq, ak, av = project(hs, w_in[l], pos_s)
        ret_o, s_new = retention_sample(rq, rk, rv, state_ret[l].astype(jnp.float32), log_g)
        att_o, k_buf, v_buf = swa_sample(aq, ak, av, cache_k[l], cache_v[l], att_sinks[l])
        xs = xs + merge(ret_o, rg, ret_gn[l], att_o, w_out[l]).astype(xs.dtype)
        sk.append(k_buf.astype(cache_k.dtype))
        sv.append(v_buf.astype(cache_v.dtype))
        ss.append(s_new.astype(state_ret.dtype))
        hp = rms_norm(xp, ln2[l])
        hs = rms_norm(xs, ln2[l])
        j = l // 2
        if l % 2 == 0:
            xp = xp + swiglu(hp, ffn_w_gate[j], ffn_w_up[j], ffn_w_down[j]).astype(xp.dtype)
            xs = xs + swiglu(hs, ffn_w_gate[j], ffn_w_up[j], ffn_w_down[j]).astype(xs.dtype)
        else:
            xp = xp + moe_ffn(hp, router[j], moe_w_gate[j], moe_w_up[j], moe_w_down[j]).astype(xp.dtype)
            xs = xs + moe_ffn(hs, router[j], moe_w_gate[j], moe_w_up[j], moe_w_down[j]).astype(xs.dtype)
    y_prompt = rms_norm(xp, ln_f)
    y_sample = rms_norm(xs, ln_f)
    return (y_prompt, y_sample, jnp.stack(pk), jnp.stack(pv), jnp.stack(ps), jnp.stack(sk), jnp.stack(sv), jnp.stack(ss))
```

```python
import functools
import math

import jax
import jax.numpy as jnp
from jax import lax
from jax.experimental import pallas as pl
from jax.experimental.pallas import tpu as pltpu

F32 = jnp.float32
BF16 = jnp.bfloat16

CHUNK = 64
RET_HEADS = 8
RET_DK = 128
RET_DV = 256
RET_THETA = 10000.0
ATT_HEADS = 32
ATT_KV_HEADS = 4
ATT_GROUP = ATT_HEADS // ATT_KV_HEADS
ATT_HEAD_DIM = 64
WINDOW = 128
ROPE_THETA = 500000.0
ROPE_DIM = ATT_HEAD_DIM // 4
TOP_K = 2
EPS = 1e-5
NEG_INF = -1e30
PAST_LEN = 2048

RET_QK_W = RET_HEADS * RET_DK
RET_V_W = RET_HEADS * RET_DV
ATT_Q_W = ATT_HEADS * ATT_HEAD_DIM
ATT_KV_W = ATT_KV_HEADS * ATT_HEAD_DIM
OFF_RQ = 0
OFF_RK = RET_QK_W
OFF_RV = 2 * RET_QK_W
OFF_RG = OFF_RV + RET_V_W
OFF_AQ = OFF_RG + RET_V_W
OFF_AK = OFF_AQ + ATT_Q_W
OFF_AV = OFF_AK + ATT_KV_W
IN_W = OFF_AV + ATT_KV_W
MIX_W = RET_V_W + ATT_Q_W

LANES = 128
VMEM_LIMIT = 60 * 1024 * 1024
ROW_BLOCK = 2304
DOWN_ROW_BLOCK = 1536
MOE_CAP = 2560
ROUTER_PAD = LANES


def _divisor_block(n, target, mult):
    if n <= target:
        return n
    b = (target // mult) * mult
    while b > mult and n % b:
        b -= mult
    assert n % b == 0, (n, target, mult)
    return b


def _params(sem):
    return pltpu.CompilerParams(dimension_semantics=sem, vmem_limit_bytes=VMEM_LIMIT)


def _sigmoid(x):
    return 1.0 / (1.0 + jnp.exp(-x))


def _norm_kernel(x_ref, g_ref, o_ref):
    x = x_ref[...]
    y = x * lax.rsqrt(jnp.mean(x * x, axis=-1, keepdims=True) + EPS)
    o_ref[...] = (y * g_ref[...]).astype(o_ref.dtype)


def _rms_norm(x, g, out_dtype, *, rows=None, row_off=0):
    m_total, d = x.shape
    rows = m_total if rows is None else rows
    bm = _divisor_block(math.gcd(rows, row_off) if row_off else rows, 256, 8)
    off = row_off // bm
    return pl.pallas_call(
        _norm_kernel,
        grid=(rows // bm,),
        in_specs=[pl.BlockSpec((bm, d), lambda i: (i + off, 0)),
                  pl.BlockSpec((1, d), lambda i: (0, 0))],
        out_specs=pl.BlockSpec((bm, d), lambda i: (i, 0)),
        out_shape=jax.ShapeDtypeStruct((rows, d), out_dtype),
        compiler_params=_params(("parallel",)),
        name="rms_norm",
    )(x, g.reshape(1, d))


def _mm_kernel(a_ref, w_ref, o_ref):
    acc = jnp.dot(a_ref[...], w_ref[...].astype(BF16), preferred_element_type=F32)
    o_ref[...] = acc.astype(o_ref.dtype)


def _mm_res_kernel(a_ref, w_ref, r_ref, o_ref):
    acc = jnp.dot(a_ref[...], w_ref[...].astype(BF16), preferred_element_type=F32)
    o_ref[...] = (r_ref[...] + acc).astype(o_ref.dtype)


def _matmul(a, w, residual=None, out_dtype=F32):
    m, k = a.shape
    n = w.shape[1]
    bm = _divisor_block(m, ROW_BLOCK, 8)
    bn = _divisor_block(n, 512, LANES)
    in_specs = [pl.BlockSpec((bm, k), lambda i, j: (i, 0), pipeline_mode=pl.Buffered(1)),
                pl.BlockSpec((k, bn), lambda i, j: (0, j))]
    args = [a, w]
    body = _mm_kernel
    if residual is not None:
        in_specs.append(pl.BlockSpec((bm, bn), lambda i, j: (i, j)))
        args.append(residual)
        body = _mm_res_kernel
    return pl.pallas_call(
        body,
        grid=(m // bm, n // bn),
        in_specs=in_specs,
        out_specs=pl.BlockSpec((bm, bn), lambda i, j: (i, j)),
        out_shape=jax.ShapeDtypeStruct((m, n), out_dtype),
        compiler_params=_params(("parallel", "arbitrary")),
        name="matmul",
    )(*args)


def _rot_tables(pos):
    posf = pos.astype(F32)[:, None]
    half = RET_DK // 2
    inv = jnp.exp(-math.log(RET_THETA) * jnp.arange(half, dtype=F32) / half)
    ang = posf * inv[None, :]
    cos, sin = jnp.cos(ang), jnp.sin(ang)
    ret_cos = jnp.concatenate([cos, cos], axis=-1)
    ret_sin = jnp.concatenate([-sin, sin], axis=-1)
    half_a = ROPE_DIM // 2
    inv_a = jnp.exp(-math.log(ROPE_THETA) * jnp.arange(half_a, dtype=F32) / half_a)
    ang_a = posf * inv_a[None, :]
    cos_a, sin_a = jnp.cos(ang_a), jnp.sin(ang_a)
    p = pos.shape[0]
    rest = ATT_HEAD_DIM - ROPE_DIM
    c = jnp.concatenate([cos_a, cos_a, jnp.ones((p, rest), F32)], axis=-1)
    s_hi = jnp.concatenate([-sin_a, jnp.zeros((p, ATT_HEAD_DIM - half_a), F32)], axis=-1)
    s_lo = jnp.concatenate([jnp.zeros((p, half_a), F32), sin_a, jnp.zeros((p, rest), F32)], axis=-1)
    reps = LANES // ATT_HEAD_DIM
    att = [jnp.tile(t, (1, reps)) for t in (c, s_hi, s_lo)]
    return ret_cos, ret_sin, att[0], att[1], att[2]


def _decay_tables(length):
    log_g = jnp.log(1.0 - jnp.exp2(-5.0 - jnp.arange(RET_HEADS, dtype=F32)))
    idx = jnp.arange(length, dtype=F32)
    intra = jnp.exp(log_g[:, None, None] * jnp.abs(idx[:, None] - idx[None, :]))
    w_in = jnp.exp(log_g[:, None] * (idx + 1.0)[None, :])[:, :, None]
    w_end = jnp.exp(log_g[:, None] * (length - 1.0 - idx)[None, :])[:, :, None]
    g_len = jnp.exp(log_g * length)[:, None, None]
    return intra, w_in, w_end, g_len


def _rot_ret(x, cos, sin):
    return x * cos + pltpu.roll(x, RET_DK // 2, 1) * sin


def _rot_att(x, c, s_hi, s_lo):
    half = ROPE_DIM // 2
    return x * c + pltpu.roll(x, LANES - half, 1) * s_hi + pltpu.roll(x, half, 1) * s_lo


_NT = (((1,), (1,)), ((), ()))
_TN = (((0,), (0,)), ((), ()))


def _retention_heads(z_ref, y_ref, cos, sin, intra_ref, win_ref, wend_ref, glen_ref, gn_ref,
                     state_in, state_out):
    for h in range(RET_HEADS):
        q = _rot_ret(z_ref[:, OFF_RQ + h * RET_DK:OFF_RQ + (h + 1) * RET_DK], cos, sin)
        k = _rot_ret(z_ref[:, OFF_RK + h * RET_DK:OFF_RK + (h + 1) * RET_DK], cos, sin) * (RET_DK ** -0.5)
        v = z_ref[:, OFF_RV + h * RET_DV:OFF_RV + (h + 1) * RET_DV]
        gate = z_ref[:, OFF_RG + h * RET_DV:OFF_RG + (h + 1) * RET_DV]
        st = state_in[h]
        s = lax.dot_general(q, k, _NT, preferred_element_type=F32) * intra_ref[h]
        o = (jnp.dot(s, v, preferred_element_type=F32)
             + jnp.dot(q, st, preferred_element_type=F32) * win_ref[h])
        kv = lax.dot_general(k * wend_ref[h], v, _TN, preferred_element_type=F32)
        state_out[h] = glen_ref[h] * st + kv
        mu = jnp.mean(o, axis=-1, keepdims=True)
        d = o - mu
        var = jnp.mean(d * d, axis=-1, keepdims=True)
        yn = d * lax.rsqrt(var + EPS) * gn_ref[:, h * RET_DV:(h + 1) * RET_DV]
        y_ref[:, h * RET_DV:(h + 1) * RET_DV] = (gate * _sigmoid(gate) * yn).astype(y_ref.dtype)


def _attention_heads(z_ref, y_ref, k_all, v_all, tabs, sink_ref, valid):
    rows = z_ref.shape[0]
    per_kv = ATT_GROUP * ATT_HEAD_DIM
    for kh in range(ATT_KV_HEADS):
        pieces = []
        for j in range(per_kv // LANES):
            lo = OFF_AQ + kh * per_kv + j * LANES
            xb = _rot_att(z_ref[:, lo:lo + LANES], *tabs)
            for t in range(LANES // ATT_HEAD_DIM):
                pieces.append(xb[:, t * ATT_HEAD_DIM:(t + 1) * ATT_HEAD_DIM])
        q = jnp.concatenate(pieces, axis=0)
        kk = k_all[:, kh * ATT_HEAD_DIM:(kh + 1) * ATT_HEAD_DIM]
        vv = v_all[:, kh * ATT_HEAD_DIM:(kh + 1) * ATT_HEAD_DIM]
        s = lax.dot_general(q, kk, _NT, preferred_element_type=F32) * (ATT_HEAD_DIM ** -0.5)
        if valid is not None:
            s = jnp.where(valid, s, NEG_INF)
        sink = sink_ref[kh]
        m = jnp.maximum(jnp.max(s, axis=-1, keepdims=True), sink)
        e = jnp.exp(s - m)
        den = jnp.sum(e, axis=-1, keepdims=True) + jnp.exp(sink - m)
        o = jnp.dot(e / den, vv, preferred_element_type=F32)
        out = jnp.concatenate([o[g * rows:(g + 1) * rows] for g in range(ATT_GROUP)], axis=1)
        y_ref[:, RET_V_W + kh * per_kv:RET_V_W + (kh + 1) * per_kv] = out.astype(y_ref.dtype)


def _new_keys(z_ref, tabs):
    parts = [_rot_att(z_ref[:, OFF_AK + j * LANES:OFF_AK + (j + 1) * LANES], *tabs)
             for j in range(ATT_KV_W // LANES)]
    return jnp.concatenate(parts, axis=1), z_ref[:, OFF_AV:OFF_AV + ATT_KV_W]


def _mix_prompt_kernel(z_ref, rc_ref, rs_ref, ac_ref, ah_ref, al_ref, intra_ref, win_ref, wend_ref,
                       glen_ref, gn_ref, sink_ref, y_ref, ko_ref, vo_ref, st_ref, kprev, vprev):
    c = pl.program_id(0)

    @pl.when(c == 0)
    def _():
        st_ref[...] = jnp.zeros_like(st_ref)
        kprev[...] = jnp.zeros_like(kprev)
        vprev[...] = jnp.zeros_like(vprev)

    _retention_heads(z_ref, y_ref, rc_ref[...], rs_ref[...], intra_ref, win_ref, wend_ref, glen_ref,
                     gn_ref, st_ref, st_ref)
    tabs = (ac_ref[...], ah_ref[...], al_ref[...])
    k_new, v_new = _new_keys(z_ref, tabs)
    ko_ref[...] = k_new
    vo_ref[...] = v_new
    k_all = jnp.concatenate([kprev[...], k_new], axis=0)
    v_all = jnp.concatenate([vprev[...], v_new], axis=0)
    col = lax.broadcasted_iota(jnp.int32, (1, WINDOW + CHUNK), 1)
    valid = col >= (WINDOW // CHUNK - c) * CHUNK
    _attention_heads(z_ref, y_ref, k_all, v_all, tabs, sink_ref, valid)
    kprev[...] = k_all[CHUNK:]
    vprev[...] = v_all[CHUNK:]


def _mix_prompt(z_all, seq, tabs, decay, gn, sinks_col):
    n = seq // CHUNK
    const2 = lambda c: (0, 0)
    const3 = lambda c: (0, 0, 0)
    row = lambda c: (c, 0)
    in_specs = ([pl.BlockSpec((CHUNK, IN_W), row)]
                + [pl.BlockSpec((CHUNK, LANES), row)] * 5
                + [pl.BlockSpec(decay[0].shape, const3), pl.BlockSpec(decay[1].shape, const3),
                   pl.BlockSpec(decay[2].shape, const3), pl.BlockSpec(decay[3].shape, const3),
                   pl.BlockSpec((1, RET_V_W), const2),
                   pl.BlockSpec(sinks_col.shape, const3)])
    return pl.pallas_call(
        _mix_prompt_kernel,
        grid=(n,),
        in_specs=in_specs,
        out_specs=[pl.BlockSpec((CHUNK, MIX_W), row),
                   pl.BlockSpec((CHUNK, ATT_KV_W), row),
                   pl.BlockSpec((CHUNK, ATT_KV_W), row),
                   pl.BlockSpec((RET_HEADS, RET_DK, RET_DV), const3)],
        out_shape=[jax.ShapeDtypeStruct((seq, MIX_W), BF16),
                   jax.ShapeDtypeStruct((seq, ATT_KV_W), F32),
                   jax.ShapeDtypeStruct((seq, ATT_KV_W), F32),
                   jax.ShapeDtypeStruct((RET_HEADS, RET_DK, RET_DV), F32)],
        scratch_shapes=[pltpu.VMEM((WINDOW, ATT_KV_W), F32), pltpu.VMEM((WINDOW, ATT_KV_W), F32)],
        compiler_params=_params(("arbitrary",)),
        name="mix_prompt",
    )(z_all, *tabs, *decay, gn, sinks_col)


def _mix_sample_kernel(z_ref, rc_ref, rs_ref, ac_ref, ah_ref, al_ref, intra_ref, win_ref, wend_ref,
                       glen_ref, gn_ref, sink_ref, st_in_ref, ck_ref, cv_ref,
                       y_ref, ko_ref, vo_ref, st_ref):
    _retention_heads(z_ref, y_ref, rc_ref[...], rs_ref[...], intra_ref, win_ref, wend_ref, glen_ref,
                     gn_ref, st_in_ref, st_ref)
    tabs = (ac_ref[...], ah_ref[...], al_ref[...])
    k_new, v_new = _new_keys(z_ref, tabs)
    k_all = jnp.concatenate([ck_ref[...], k_new], axis=0)
    v_all = jnp.concatenate([cv_ref[...], v_new], axis=0)
    _attention_heads(z_ref, y_ref, k_all, v_all, tabs, sink_ref, None)
    win = ko_ref.shape[0]
    ko_ref[...] = k_all[k_all.shape[0] - win:]
    vo_ref[...] = v_all[v_all.shape[0] - win:]


def _mix_sample(z_all, seq, bd, length, tabs, decay, gn, sinks_col, state, ck, cv):
    win = ck.shape[1]
    off = seq // length
    const2 = lambda b: (0, 0)
    const3 = lambda b: (0, 0, 0)
    in_specs = ([pl.BlockSpec((length, IN_W), lambda b: (b + off, 0))]
                + [pl.BlockSpec((length, LANES), const2)] * 5
                + [pl.BlockSpec(decay[0].shape, const3), pl.BlockSpec(decay[1].shape, const3),
                   pl.BlockSpec(decay[2].shape, const3), pl.BlockSpec(decay[3].shape, const3),
                   pl.BlockSpec((1, RET_V_W), const2),
                   pl.BlockSpec(sinks_col.shape, const3),
                   pl.BlockSpec((None, RET_HEADS, RET_DK, RET_DV), lambda b: (b, 0, 0, 0)),
                   pl.BlockSpec((None, win, ATT_KV_W), lambda b: (b, 0, 0)),
                   pl.BlockSpec((None, win, ATT_KV_W), lambda b: (b, 0, 0))])
    return pl.pallas_call(
        _mix_sample_kernel,
        grid=(bd,),
        in_specs=in_specs,
        out_specs=[pl.BlockSpec((length, MIX_W), lambda b: (b, 0)),
                   pl.BlockSpec((None, win, ATT_KV_W), lambda b: (b, 0, 0)),
                   pl.BlockSpec((None, win, ATT_KV_W), lambda b: (b, 0, 0)),
                   pl.BlockSpec((None, RET_HEADS, RET_DK, RET_DV), lambda b: (b, 0, 0, 0))],
        out_shape=[jax.ShapeDtypeStruct((bd * length, MIX_W), BF16),
                   jax.ShapeDtypeStruct(ck.shape, F32),
                   jax.ShapeDtypeStruct(cv.shape, F32),
                   jax.ShapeDtypeStruct(state.shape, F32)],
        compiler_params=_params(("parallel",)),
        name="mix_sample",
    )(z_all, *tabs, *decay, gn, sinks_col, state, ck, cv)


def _gate_up_kernel(be_ref, nu_ref, a_ref, wg_ref, wu_ref, o_ref):
    del be_ref

    @pl.when(pl.program_id(0) < nu_ref[0])
    def _():
        a = a_ref[...]
        g = jnp.dot(a, wg_ref[...].astype(BF16), preferred_element_type=F32)
        u = jnp.dot(a, wu_ref[...].astype(BF16), preferred_element_type=F32)
        o_ref[...] = (g * _sigmoid(g) * u).astype(o_ref.dtype)


def _gate_up(xs, wg, wu, blk_expert, n_used, cap):
    r, d = xs.shape
    f = wg.shape[2]
    nblk = r // cap
    bn = _divisor_block(f, 256, LANES)
    nj = f // bn

    def blk(w, nu):
        return jnp.minimum(w, nu[0] - 1)

    def col(w, j, nu):
        return jnp.where(w < nu[0], j, nj - 1)

    grid_spec = pltpu.PrefetchScalarGridSpec(
        num_scalar_prefetch=2,
        grid=(nblk, nj),
        in_specs=[pl.BlockSpec((cap, d), lambda w, j, be, nu: (blk(w, nu), 0), pipeline_mode=pl.Buffered(1)),
                  pl.BlockSpec((None, d, bn), lambda w, j, be, nu: (be[w], 0, col(w, j, nu))),
                  pl.BlockSpec((None, d, bn), lambda w, j, be, nu: (be[w], 0, col(w, j, nu)))],
        out_specs=pl.BlockSpec((cap, bn), lambda w, j, be, nu: (blk(w, nu), col(w, j, nu))),
    )
    return pl.pallas_call(
        _gate_up_kernel,
        grid_spec=grid_spec,
        out_shape=jax.ShapeDtypeStruct((r, f), BF16),
        compiler_params=_params(("arbitrary", "arbitrary")),
        name="ffn_gate_up",
    )(blk_expert, n_used, xs, wg, wu)


def _down_kernel(be_ref, nu_ref, t_ref, wd_ref, e_ref, o_ref, *, add):
    del be_ref
    k = pl.program_id(2)

    @pl.when(pl.program_id(0) < nu_ref[0])
    def _():
        @pl.when(k == 0)
        def _():
            o_ref[...] = jnp.zeros_like(o_ref)

        o_ref[...] += jnp.dot(t_ref[...], wd_ref[...].astype(BF16), preferred_element_type=F32)

        @pl.when(k == pl.num_programs(2) - 1)
        def _():
            if add:
                o_ref[...] = e_ref[...] + o_ref[...]
            else:
                o_ref[...] = e_ref[...] * o_ref[...]


def _down(t, wd, extra, blk_expert, n_used, cap, *, add):
    r, f = t.shape
    d = wd.shape[2]
    nblk = r // cap
    bn = _divisor_block(d, 1024, LANES)
    bk = _divisor_block(f, 1024, LANES)
    nn, nk = d // bn, f // bk

    def blk(w, nu):
        return jnp.minimum(w, nu[0] - 1)

    def sel(w, x, last, nu):
        return jnp.where(w < nu[0], x, last)

    e_cols = bn if add else 1
    grid_spec = pltpu.PrefetchScalarGridSpec(
        num_scalar_prefetch=2,
        grid=(nblk, nn, nk),
        in_specs=[pl.BlockSpec((cap, bk), lambda w, n, k, be, nu: (blk(w, nu), sel(w, k, nk - 1, nu))),
                  pl.BlockSpec((None, bk, bn),
                               lambda w, n, k, be, nu: (be[w], sel(w, k, nk - 1, nu), sel(w, n, nn - 1, nu))),
                  pl.BlockSpec((cap, e_cols),
                               lambda w, n, k, be, nu: (blk(w, nu), sel(w, n, nn - 1, nu) if add else 0))],
        out_specs=pl.BlockSpec((cap, bn), lambda w, n, k, be, nu: (blk(w, nu), sel(w, n, nn - 1, nu))),
    )
    return pl.pallas_call(
        functools.partial(_down_kernel, add=add),
        grid_spec=grid_spec,
        out_shape=jax.ShapeDtypeStruct((r, d), F32),
        compiler_params=_params(("arbitrary", "arbitrary", "arbitrary")),
        name="ffn_down",
    )(blk_expert, n_used, t, wd, extra)


def _router_kernel(x_ref, g_ref, wr_ref, idx_ref, gate_ref, *, n_experts):
    x = x_ref[...]
    h = x * lax.rsqrt(jnp.mean(x * x, axis=-1, keepdims=True) + EPS) * g_ref[...]
    logits = jnp.dot(h, wr_ref[...], preferred_element_type=F32, precision=lax.Precision.HIGHEST)
    lane = lax.broadcasted_iota(jnp.int32, logits.shape, 1)
    logits = jnp.where(lane < n_experts, logits, -jnp.inf)
    v1 = jnp.max(logits, axis=-1, keepdims=True)
    i1 = jnp.min(jnp.where(logits == v1, lane, ROUTER_PAD), axis=-1, keepdims=True)
    rest = jnp.where(lane == i1, -jnp.inf, logits)
    v2 = jnp.max(rest, axis=-1, keepdims=True)
    i2 = jnp.min(jnp.where(rest == v2, lane, ROUTER_PAD), axis=-1, keepdims=True)
    e2 = jnp.exp(v2 - v1)
    den = 1.0 + e2
    idx_ref[...] = jnp.where(lane == 0, i1, jnp.where(lane == 1, i2, 0))
    gate_ref[...] = jnp.where(lane == 0, 1.0 / den, jnp.where(lane == 1, e2 / den, 0.0))


def _route(x, g, w_router):
    m, d = x.shape
    n_experts = w_router.shape[1]
    wr = jnp.pad(w_router, ((0, 0), (0, ROUTER_PAD - n_experts)))
    bm = _divisor_block(m, 256, 8)
    idx, gate = pl.pallas_call(
        functools.partial(_router_kernel, n_experts=n_experts),
        grid=(m // bm,),
        in_specs=[pl.BlockSpec((bm, d), lambda i: (i, 0)),
                  pl.BlockSpec((1, d), lambda i: (0, 0)),
                  pl.BlockSpec((d, ROUTER_PAD), lambda i: (0, 0))],
        out_specs=[pl.BlockSpec((bm, ROUTER_PAD), lambda i: (i, 0)),
                   pl.BlockSpec((bm, ROUTER_PAD), lambda i: (i, 0))],
        out_shape=[jax.ShapeDtypeStruct((m, ROUTER_PAD), jnp.int32),
                   jax.ShapeDtypeStruct((m, ROUTER_PAD), F32)],
        compiler_params=_params(("parallel",)),
        name="moe_router",
    )(x, g.reshape(1, d), wr)
    return idx[:, :TOP_K], gate[:, :TOP_K]


def _moe_layout(idx, gate, n_experts, cap, nblk):
    m = idx.shape[0]
    e_flat = idx.reshape(-1)
    onehot = (e_flat[:, None] == jnp.arange(n_experts, dtype=jnp.int32)[None, :]).astype(jnp.int32)
    csum = jnp.cumsum(onehot, axis=0)
    rank = jnp.sum((csum - onehot) * onehot, axis=1)
    counts = csum[-1]
    blocks = (counts + cap - 1) // cap
    blk_end = jnp.cumsum(blocks)
    n_used = blk_end[-1]
    slot = (blk_end - blocks)[e_flat] * cap + rank
    w = jnp.arange(nblk, dtype=jnp.int32)
    blk_expert = jnp.searchsorted(blk_end, jnp.minimum(w, n_used - 1), side="right").astype(jnp.int32)
    token = jnp.arange(m * TOP_K, dtype=jnp.int32) // TOP_K
    src = jnp.zeros((nblk * cap,), jnp.int32).at[slot].set(token)
    gate_slot = jnp.zeros((nblk * cap,), F32).at[slot].set(gate.reshape(-1))
    return slot.reshape(m, TOP_K).astype(jnp.int32), src, gate_slot, blk_expert, n_used.reshape(1).astype(jnp.int32)


def _gather_norm_kernel(src_ref, x_hbm, g_ref, o_ref, buf, sem):
    rows = buf.shape[0]

    def copy(r, tok):
        return pltpu.make_async_copy(x_hbm.at[pl.ds(tok, 1), :], buf.at[pl.ds(r, 1), :], sem)

    def start(r, carry):
        copy(r, src_ref[r]).start()
        return carry

    def wait(r, carry):
        copy(r, 0).wait()
        return carry

    lax.fori_loop(0, rows, start, 0)
    lax.fori_loop(0, rows, wait, 0)
    x = buf[...]
    y = x * lax.rsqrt(jnp.mean(x * x, axis=-1, keepdims=True) + EPS)
    o_ref[...] = (y * g_ref[...]).astype(o_ref.dtype)


def _gather_norm(x, g, src):
    d = x.shape[1]
    r = src.shape[0]
    gm = _divisor_block(r, 256, 16)
    return pl.pallas_call(
        _gather_norm_kernel,
        grid=(r // gm,),
        in_specs=[pl.BlockSpec((gm,), lambda i: (i,), memory_space=pltpu.SMEM),
                  pl.BlockSpec(memory_space=pl.ANY),
                  pl.BlockSpec((1, d), lambda i: (0, 0))],
        out_specs=pl.BlockSpec((gm, d), lambda i: (i, 0)),
        out_shape=jax.ShapeDtypeStruct((r, d), BF16),
        scratch_shapes=[pltpu.VMEM((gm, d), F32), pltpu.SemaphoreType.DMA(())],
        compiler_params=_params(("arbitrary",)),
        name="moe_gather_norm",
    )(src, x, g.reshape(1, d))


def _combine_kernel(sa_ref, sb_ref, y_hbm, x_ref, o_ref, buf_a, buf_b, sem_a, sem_b):
    rows = buf_a.shape[0]

    def copy(buf, sem, r, slot):
        return pltpu.make_async_copy(y_hbm.at[pl.ds(slot, 1), :], buf.at[pl.ds(r, 1), :], sem)

    def start(r, carry):
        copy(buf_a, sem_a, r, sa_ref[r]).start()
        copy(buf_b, sem_b, r, sb_ref[r]).start()
        return carry

    def wait(r, carry):
        copy(buf_a, sem_a, r, 0).wait()
        copy(buf_b, sem_b, r, 0).wait()
        return carry

    lax.fori_loop(0, rows, start, 0)
    lax.fori_loop(0, rows, wait, 0)
    o_ref[...] = x_ref[...] + (buf_a[...] + buf_b[...])


def _combine(x, y, slot):
    m, d = x.shape
    cm = _divisor_block(m, 128, 8)
    return pl.pallas_call(
        _combine_kernel,
        grid=(m // cm,),
        in_specs=[pl.BlockSpec((cm,), lambda i: (i,), memory_space=pltpu.SMEM),
                  pl.BlockSpec((cm,), lambda i: (i,), memory_space=pltpu.SMEM),
                  pl.BlockSpec(memory_space=pl.ANY),
                  pl.BlockSpec((cm, d), lambda i: (i, 0))],
        out_specs=pl.BlockSpec((cm, d), lambda i: (i, 0)),
        out_shape=jax.ShapeDtypeStruct((m, d), F32),
        scratch_shapes=[pltpu.VMEM((cm, d), F32), pltpu.VMEM((cm, d), F32),
                        pltpu.SemaphoreType.DMA(()), pltpu.SemaphoreType.DMA(())],
        compiler_params=_params(("arbitrary",)),
        name="moe_combine",
    )(slot[:, 0], slot[:, 1], y, x)


def _dense_ffn(x_all, g, wg, wu, wd):
    m, d = x_all.shape
    h = _rms_norm(x_all, g, BF16)

    def blocks(cap):
        nblk = m // cap
        return jnp.zeros((nblk,), jnp.int32), jnp.full((1,), nblk, jnp.int32), cap

    t = _gate_up(h, wg, wu, *blocks(_divisor_block(m, ROW_BLOCK, 16)))
    return _down(t, wd, x_all, *blocks(_divisor_block(m, DOWN_ROW_BLOCK, 16)), add=True)


def _moe_ffn(x_all, g, w_router, wg, wu, wd):
    m, d = x_all.shape
    n_experts = w_router.shape[1]
    cap = min(MOE_CAP, m * TOP_K)
    nblk = (m * TOP_K) // cap + n_experts
    idx, gate = _route(x_all, g, w_router)
    slot, src, gate_slot, blk_expert, n_used = _moe_layout(idx, gate, n_experts, cap, nblk)
    xs = _gather_norm(x_all, g, src)
    t = _gate_up(xs, wg, wu, blk_expert, n_used, cap)
    y = _down(t, wd, gate_slot[:, None], blk_expert, n_used, cap, add=False)
    return _combine(x_all, y, slot)


def kernel(x_prompt, x_sample, cache_k, cache_v, state_ret, ln1, w_in, ret_gn, att_sinks, w_out,
           ln2, ffn_w_gate, ffn_w_up, ffn_w_down, router, moe_w_gate, moe_w_up, moe_w_down, ln_f):
    _, seq, d = x_prompt.shape
    bd, length, _ = x_sample.shape
    depth = ln1.shape[0]
    win = cache_k.shape[2]
    ms = bd * length
    assert seq % CHUNK == 0 and seq % length == 0 and x_prompt.shape[0] == 1

    x_all = jnp.concatenate([x_prompt.reshape(seq, d), x_sample.reshape(ms, d)], axis=0)
    tabs_p = _rot_tables(jnp.arange(seq, dtype=jnp.int32))
    tabs_s = _rot_tables(PAST_LEN + jnp.arange(length, dtype=jnp.int32))
    decay_p = _decay_tables(CHUNK)
    decay_s = _decay_tables(length)

    pk, pv, ps, sk, sv, ss = [], [], [], [], [], []
    for l in range(depth):
        h = _rms_norm(x_all, ln1[l], BF16)
        z = _matmul(h, w_in[l])
        gn = ret_gn[l].reshape(1, RET_V_W)
        sinks = att_sinks[l].reshape(ATT_KV_HEADS, ATT_GROUP, 1).astype(F32)
        sinks_p = jnp.repeat(sinks, CHUNK, axis=1)
        sinks_s = jnp.repeat(sinks, length, axis=1)
        y_p, k_rot, v_new, s_fin = _mix_prompt(z, seq, tabs_p, decay_p, gn, sinks_p)
        y_s, k_buf, v_buf, s_new = _mix_sample(
            z, seq, bd, length, tabs_s, decay_s, gn, sinks_s, state_ret[l],
            cache_k[l].reshape(bd, win, ATT_KV_W), cache_v[l].reshape(bd, win, ATT_KV_W))
        x_all = _matmul(jnp.concatenate([y_p, y_s], axis=0), w_out[l], residual=x_all)
        win_p = min(WINDOW, seq)
        pk.append(k_rot[seq - win_p:].reshape(1, win_p, ATT_KV_HEADS, ATT_HEAD_DIM))
        pv.append(v_new[seq - win_p:].reshape(1, win_p, ATT_KV_HEADS, ATT_HEAD_DIM))
        ps.append(s_fin[None])
        sk.append(k_buf.reshape(bd, win, ATT_KV_HEADS, ATT_HEAD_DIM))
        sv.append(v_buf.reshape(bd, win, ATT_KV_HEADS, ATT_HEAD_DIM))
        ss.append(s_new)
        j = l // 2
        if l % 2 == 0:
            x_all = _dense_ffn(x_all, ln2[l], ffn_w_gate[j:j + 1], ffn_w_up[j:j + 1], ffn_w_down[j:j + 1])
        else:
            x_all = _moe_ffn(x_all, ln2[l], router[j], moe_w_gate[j], moe_w_up[j], moe_w_down[j])

    y_prompt = _rms_norm(x_all, ln_f, F32, rows=seq).reshape(1, seq, d)
    y_sample = _rms_norm(x_all, ln_f, F32, rows=ms, row_off=seq).reshape(bd, length, d)
    return (y_prompt, y_sample, jnp.stack(pk), jnp.stack(pv), jnp.stack(ps),
            jnp.stack(sk), jnp.stack(sv), jnp.stack(ss))
```
